```python
import math
import jax, jax.numpy as jnp
from jax import lax
import numpy as np

D_MODEL = 1024
BATCH = 8
SEQ = 8192
DEPTH = 1
DEC_BATCH = 32
DEC_SEQ = 32
PAST_LEN = 4096

CHUNK = 64
QBLK = 128
HEAD_DIM = 128
H_DIFF = 4
H_SB = 4
DK_DIFF = HEAD_DIM // 2
DIFF_W = H_DIFF * HEAD_DIM
SB_W = H_SB * HEAD_DIM
MIX_WIDTH = DIFF_W + SB_W
IN_WIDTH = 3 * DIFF_W + 3 * SB_W
N_BUCKETS = 32
MAX_DISTANCE = 128
N_EXPERTS = 64
N_GROUPS = 8
TOPK_GROUPS = 4
TOP_K = 6
D_EXPERT = 256
D_SHARED = 256
EBLK = 256
ROUTED_SCALE = 2.5
LN_EPS = 1e-5
RMS_EPS = 1e-5
NEG_INF = -1e30
ALPHA = (2.0 * DEPTH) ** 0.25
BETA = (8.0 * DEPTH) ** -0.25

kernel_name = "hymba_diffattn_stickbreak_moe_stream"


def layer_norm(x, g, b):
    xf = x.astype(jnp.float32)
    mu = jnp.mean(xf, axis=-1, keepdims=True)
    var = jnp.mean(jnp.square(xf - mu), axis=-1, keepdims=True)
    return ((xf - mu) * lax.rsqrt(var + LN_EPS) * g + b).astype(x.dtype)


def rms_heads(o, w):
    of = o.astype(jnp.float32)
    of = of * lax.rsqrt(jnp.mean(jnp.square(of), axis=-1, keepdims=True) + RMS_EPS)
    return (of * w).astype(o.dtype)


def t5_bucket(rel):
    nb = N_BUCKETS // 2
    ret = jnp.where(rel > 0, nb, 0)
    n = jnp.abs(rel)
    max_exact = nb // 2
    nf = jnp.maximum(n, 1).astype(jnp.float32)
    large = max_exact + (jnp.log(nf / max_exact) / math.log(MAX_DISTANCE / max_exact)
                         * (nb - max_exact)).astype(jnp.int32)
    large = jnp.minimum(large, nb - 1)
    return ret + jnp.where(n < max_exact, n, large)


def rel_bias(table, q_pos, k_pos):
    buckets = t5_bucket(k_pos[None, :] - q_pos[:, None])
    return jnp.transpose(table[buckets].astype(jnp.float32), (2, 0, 1))


def diff_block(q, k, v, q_pos, k_pos, table, lam):
    bias = rel_bias(table, q_pos, k_pos)[None]
    mask = (k_pos[None, :] // CHUNK) <= (q_pos[:, None] // CHUNK)
    scale = DK_DIFF ** -0.5

    def probs(qa, ka):
        s = jnp.einsum('bqhd,bkhd->bhqk', qa, ka, preferred_element_type=jnp.float32) * scale + bias
        return jax.nn.softmax(jnp.where(mask, s, NEG_INF), axis=-1)

    w = probs(q[..., :DK_DIFF], k[..., :DK_DIFF]) - lam * probs(q[..., DK_DIFF:], k[..., DK_DIFF:])
    return jnp.einsum('bhqk,bkhd->bqhd', w.astype(v.dtype), v)


def sb_block(q, k, v, q_pos, k_pos):
    z = jnp.einsum('bqhd,bkhd->bhqk', q, k, preferred_element_type=jnp.float32) * (HEAD_DIM ** -0.5)
    mask = k_pos[None, :] < q_pos[:, None]
    log_keep = jnp.where(mask, jax.nn.log_sigmoid(-z), 0.0)
    after = lax.cumsum(log_keep, axis=3, reverse=True) - log_keep
    a = jnp.where(mask, jnp.exp(jax.nn.log_sigmoid(z) + after), 0.0)
    return jnp.einsum('bhqk,bkhd->bqhd', a.astype(v.dtype), v)


def sweep(fn, q, keys, past_len):
    B, L = q.shape[:2]
    if past_len == 0 and L > QBLK:
        outs = []
        for i in range(L // QBLK):
            lo, hi = i * QBLK, (i + 1) * QBLK
            outs.append(fn(q[:, lo:hi], tuple(k[:, :hi] for k in keys),
                           jnp.arange(lo, hi, dtype=jnp.int32), jnp.arange(hi, dtype=jnp.int32)))
        return jnp.concatenate(outs, axis=1)
    return fn(q, keys, past_len + jnp.arange(L, dtype=jnp.int32),
              jnp.arange(past_len + L, dtype=jnp.int32))


def token_mixer(x, past, table, w_in, lq1, lk1, lq2, lk2, subln_w, sb_w, w_out, lam_init):
    B, L, _ = x.shape
    proj = x @ w_in
    heads = lambda start, h: proj[..., start:start + h * HEAD_DIM].reshape(B, L, h, HEAD_DIM)
    qd = heads(0, H_DIFF)
    kd = heads(DIFF_W, H_DIFF)
    vd = heads(2 * DIFF_W, H_DIFF)
    qs = heads(3 * DIFF_W, H_SB)
    ks = heads(3 * DIFF_W + SB_W, H_SB)
    vs = heads(3 * DIFF_W + 2 * SB_W, H_SB)
    new_rows = (kd, vd, ks, vs)
    if past is None:
        P = 0
        keys = new_rows
    else:
        P = past[0].shape[1]
        keys = tuple(jnp.concatenate([p.astype(r.dtype), r], axis=1) for p, r in zip(past, new_rows))
    lam = (jnp.exp(jnp.sum(lq1.astype(jnp.float32) * lk1.astype(jnp.float32)))
           - jnp.exp(jnp.sum(lq2.astype(jnp.float32) * lk2.astype(jnp.float32))) + lam_init)
    od = sweep(lambda qb, kb, qp, kp: diff_block(qb, kb[0], kb[1], qp, kp, table, lam),
               qd, (keys[0], keys[1]), P)
    osb = sweep(lambda qb, kb, qp, kp: sb_block(qb, kb[0], kb[1], qp, kp),
                qs, (keys[2], keys[3]), P)
    od = rms_heads(od, subln_w) * (1.0 - lam_init)
    osb = rms_heads(osb, sb_w)
    o = jnp.concatenate([od, osb], axis=2).reshape(B, L, MIX_WIDTH)
    return o @ w_out, new_rows


def moe(h, router_w, router_bias, e_gate, e_up, e_down, s_gate, s_up, s_down):
    B, L, D = h.shape
    T = B * L
    M = T * TOP_K
    xf = h.reshape(T, D)
    scores = jax.nn.sigmoid((xf @ router_w).astype(jnp.float32))
    sel = (scores + router_bias.astype(jnp.float32)).reshape(T, N_GROUPS, N_EXPERTS // N_GROUPS)
    grp_score = lax.top_k(sel, 2)[0].sum(-1)
    _, top_grp = lax.top_k(grp_score, TOPK_GROUPS)
    grp_keep = jnp.any(top_grp[:, :, None] == jnp.arange(N_GROUPS)[None, None, :], axis=1)
    sel = jnp.where(grp_keep[:, :, None], sel, -jnp.inf).reshape(T, N_EXPERTS)
    _, idx = lax.top_k(sel, TOP_K)
    g = jnp.take_along_axis(scores, idx, axis=-1)
    g = g / (jnp.sum(g, axis=-1, keepdims=True) + 1e-20) * ROUTED_SCALE
    flat_e = idx.reshape(M)
    order = jnp.argsort(flat_e)
    sorted_e = flat_e[order]
    tok = order // TOP_K
    sizes = jnp.bincount(flat_e, length=N_EXPERTS).astype(jnp.int32)
    start = jnp.cumsum(sizes) - sizes
    padded = (sizes + EBLK - 1) // EBLK * EBLK
    pend = jnp.cumsum(padded)
    dest = (pend - padded)[sorted_e] + jnp.arange(M, dtype=jnp.int32) - start[sorted_e]
    n_blk = (M + N_EXPERTS * (EBLK - 1) + EBLK - 1) // EBLK
    buf = jnp.zeros((n_blk * EBLK, D), h.dtype).at[dest].set(xf[tok])
    blk_e = jnp.minimum(jnp.searchsorted(pend, jnp.arange(n_blk, dtype=jnp.int32) * EBLK, side='right'),
                        N_EXPERTS - 1)
    xb = buf.reshape(n_blk, EBLK, D)
    hg = jnp.einsum('ngd,ndf->ngf', xb, e_gate[blk_e])
    hu = jnp.einsum('ngd,ndf->ngf', xb, e_up[blk_e])
    yb = jnp.einsum('ngf,nfd->ngd', jax.nn.silu(hg) * hu, e_down[blk_e]).reshape(n_blk * EBLK, D)
    ys = yb[dest] * g.reshape(M)[order][:, None].astype(yb.dtype)
    routed = jax.ops.segment_sum(ys, tok, num_segments=T)
    shared = (jax.nn.silu(xf @ s_gate) * (xf @ s_up)) @ s_down
    return (routed + shared).reshape(B, L, D)


def layer(x, past, table, w_in, lq1, lk1, lq2, lk2, subln_w, sb_w, w_out, ln1_g, ln1_b,
          router_w, router_bias, e_gate, e_up, e_down, s_gate, s_up, s_down, ln2_g, ln2_b, lam_init):
    mix, rows = token_mixer(x, past, table, w_in, lq1, lk1, lq2, lk2, subln_w, sb_w, w_out, lam_init)
    h = layer_norm(ALPHA * x + mix, ln1_g, ln1_b)
    y = layer_norm(ALPHA * h + moe(h, router_w, router_bias, e_gate, e_up, e_down, s_gate, s_up, s_down),
                   ln2_g, ln2_b)
    return y, rows


def setup_inputs(seed: int = 0) -> dict:
    key = jax.random.key(seed)
    ks = jax.random.split(key, 28)
    nrm = lambda k, shape, s: jax.random.normal(k, shape, jnp.float32) * s
    col_scale = jnp.concatenate([jnp.ones((2 * DIFF_W,)), jnp.full((DIFF_W,), BETA),
                                 jnp.ones((2 * SB_W,)), jnp.full((SB_W,), BETA)]).astype(jnp.float32)
    return {
        "x_prompt": nrm(ks[0], (BATCH, SEQ, D_MODEL), 1.0),
        "x_sample": nrm(ks[1], (DEC_BATCH, DEC_SEQ, D_MODEL), 1.0),
        "cache_diff_k": nrm(ks[2], (DEPTH, DEC_BATCH, PAST_LEN, H_DIFF, HEAD_DIM), 1.0),
        "cache_diff_v": nrm(ks[3], (DEPTH, DEC_BATCH, PAST_LEN, H_DIFF, HEAD_DIM), BETA),
        "cache_sb_k": nrm(ks[4], (DEPTH, DEC_BATCH, PAST_LEN, H_SB, HEAD_DIM), 1.0),
        "cache_sb_v": nrm(ks[5], (DEPTH, DEC_BATCH, PAST_LEN, H_SB, HEAD_DIM), BETA),
        "rel_bias_table": nrm(ks[6], (N_BUCKETS, H_DIFF), 0.1),
        "w_in": nrm(ks[7], (DEPTH, D_MODEL, IN_WIDTH), D_MODEL ** -0.5) * col_scale,
        "diff_lambda_q1": nrm(ks[8], (DEPTH, DK_DIFF), 0.1),
        "diff_lambda_k1": nrm(ks[9], (DEPTH, DK_DIFF), 0.1),
        "diff_lambda_q2": nrm(ks[10], (DEPTH, DK_DIFF), 0.1),
        "diff_lambda_k2": nrm(ks[11], (DEPTH, DK_DIFF), 0.1),
        "diff_subln_w": 1.0 + nrm(ks[12], (DEPTH, HEAD_DIM), 0.01),
        "sb_norm_w": 1.0 + nrm(ks[13], (DEPTH, HEAD_DIM), 0.01),
        "w_out": nrm(ks[14], (DEPTH, MIX_WIDTH, D_MODEL), MIX_WIDTH ** -0.5 * BETA),
        "ln1_g": 1.0 + nrm(ks[15], (DEPTH, D_MODEL), 0.01),
        "ln1_b": nrm(ks[16], (DEPTH, D_MODEL), 0.01),
        "router_w": nrm(ks[17], (DEPTH, D_MODEL, N_EXPERTS), D_MODEL ** -0.5),
        "router_bias": nrm(ks[18], (DEPTH, N_EXPERTS), 0.01),
        "expert_w_gate": nrm(ks[19], (DEPTH, N_EXPERTS, D_MODEL, D_EXPERT), D_MODEL ** -0.5),
        "expert_w_up": nrm(ks[20], (DEPTH, N_EXPERTS, D_MODEL, D_EXPERT), D_MODEL ** -0.5),
        "expert_w_down": nrm(ks[21], (DEPTH, N_EXPERTS, D_EXPERT, D_MODEL), D_EXPERT ** -0.5 * BETA),
        "shared_w_gate": nrm(ks[22], (DEPTH, D_MODEL, D_SHARED), D_MODEL ** -0.5),
        "shared_w_up": nrm(ks[23], (DEPTH, D_MODEL, D_SHARED), D_MODEL ** -0.5),
        "shared_w_down": nrm(ks[24], (DEPTH, D_SHARED, D_MODEL), D_SHARED ** -0.5 * BETA),
        "ln2_g": 1.0 + nrm(ks[25], (DEPTH, D_MODEL), 0.01),
        "ln2_b": nrm(ks[26], (DEPTH, D_MODEL), 0.01),
    }


def reference(x_prompt, x_sample, cache_diff_k, cache_diff_v, cache_sb_k, cache_sb_v, rel_bias_table,
              w_in, diff_lambda_q1, diff_lambda_k1, diff_lambda_q2, diff_lambda_k2, diff_subln_w, sb_norm_w,
              w_out, ln1_g, ln1_b, router_w, router_bias, expert_w_gate, expert_w_up, expert_w_down,
              shared_w_gate, shared_w_up, shared_w_down, ln2_g, ln2_b):
    yp, ys = x_prompt, x_sample
    pk, pv, psk, psv = [], [], [], []
    sk, sv, ssk, ssv = [], [], [], []
    for l in range(DEPTH):
        lam_init = 0.8 - 0.6 * math.exp(-0.3 * l)
        params = (rel_bias_table, w_in[l], diff_lambda_q1[l], diff_lambda_k1[l], diff_lambda_q2[l],
                  diff_lambda_k2[l], diff_subln_w[l], sb_norm_w[l], w_out[l], ln1_g[l], ln1_b[l],
                  router_w[l], router_bias[l], expert_w_gate[l], expert_w_up[l], expert_w_down[l],
                  shared_w_gate[l], shared_w_up[l], shared_w_down[l], ln2_g[l], ln2_b[l], lam_init)
        yp, rp = layer(yp, None, *params)
        past = (cache_diff_k[l], cache_diff_v[l], cache_sb_k[l], cache_sb_v[l])
        ys, rs = layer(ys, past, *params)
        pk.append(rp[0]); pv.append(rp[1]); psk.append(rp[2]); psv.append(rp[3])
        sk.append(rs[0]); sv.append(rs[1]); ssk.append(rs[2]); ssv.append(rs[3])
    return (yp, ys, jnp.stack(pk), jnp.stack(pv), jnp.stack(psk), jnp.stack(psv),
            jnp.stack(sk), jnp.stack(sv), jnp.stack(ssk), jnp.stack(ssv))
```

```python
import functools
import math

import jax
import jax.numpy as jnp
from jax import lax
from jax.experimental import pallas as pl
from jax.experimental.pallas import tpu as pltpu

F32 = jnp.float32
BF16 = jnp.bfloat16
I32 = jnp.int32

HEAD_DIM = 128
N_HEADS = 4
DK_DIFF = HEAD_DIM // 2
MIX_W = N_HEADS * HEAD_DIM
CHUNK = 64
N_BUCKETS = 32
MAX_DISTANCE = 128
N_EXPERTS = 64
N_GROUPS = 8
GROUP_SIZE = N_EXPERTS // N_GROUPS
TOPK_GROUPS = 4
TOP_K = 6
EBLK = 256
ROUTED_SCALE = 2.5
LN_EPS = 1e-5
RMS_EPS = 1e-5
NEG_INF = -1e30

VMEM_LIMIT_BYTES = 56 * 1024 * 1024

_NT = (((1,), (1,)), ((), ()))


def _cparams(n_axes):
    return pltpu.CompilerParams(
        dimension_semantics=("arbitrary",) * n_axes, vmem_limit_bytes=VMEM_LIMIT_BYTES)


def _pick_tile(n, candidates):
    for c in candidates:
        if n % c == 0:
            return c
    raise ValueError(f"no tile in {candidates} divides {n}")


def _inproj_kernel(x_ref, w_ref, qd_ref, kd_ref, vd_ref, qs_ref, ks_ref, vs_ref,
                   kdb_ref, vdb_ref, ksb_ref, vsb_ref):
    xb = x_ref[...].astype(BF16)

    def proj(c):
        return jnp.dot(xb, w_ref[:, c * MIX_W:(c + 1) * MIX_W], preferred_element_type=F32)

    qd_ref[...] = (proj(0) * (DK_DIFF ** -0.5)).astype(BF16)
    kd = proj(1)
    kd_ref[...] = kd
    kdb_ref[...] = kd.astype(BF16)
    vd = proj(2)
    vd_ref[...] = vd
    vdb_ref[...] = vd.astype(BF16)
    qs_ref[...] = (proj(3) * (HEAD_DIM ** -0.5)).astype(BF16)
    ks = proj(4)
    ks_ref[...] = ks
    ksb_ref[...] = ks.astype(BF16)
    vs = proj(5)
    vs_ref[...] = vs
    vsb_ref[...] = vs.astype(BF16)


def _in_projection(x_all, w_in_b):
    T, D = x_all.shape
    tm = _pick_tile(T, (512, 256, 128))
    row = lambda i: (i, 0)
    o_spec = pl.BlockSpec((tm, MIX_W), row)
    f32_o = jax.ShapeDtypeStruct((T, MIX_W), F32)
    b16_o = jax.ShapeDtypeStruct((T, MIX_W), BF16)
    return pl.pallas_call(
        _inproj_kernel,
        grid=(T // tm,),
        in_specs=[pl.BlockSpec((tm, D), row), pl.BlockSpec(w_in_b.shape, lambda i: (0, 0))],
        out_specs=[o_spec] * 10,
        out_shape=[b16_o, f32_o, f32_o, b16_o, f32_o, f32_o, b16_o, b16_o, b16_o, b16_o],
        compiler_params=_cparams(1),
        name="in_projection",
    )(x_all, w_in_b)


def _t5_bucket(rel):
    nb = N_BUCKETS // 2
    ret = jnp.where(rel > 0, nb, 0)
    n = jnp.abs(rel)
    max_exact = nb // 2
    nf = jnp.maximum(n, 1).astype(F32)
    large = max_exact + (jnp.log(nf / max_exact) / math.log(MAX_DISTANCE / max_exact)
                         * (nb - max_exact)).astype(I32)
    large = jnp.minimum(large, nb - 1)
    return ret + jnp.where(n < max_exact, n, large)


def _bias_tile(table, q_pos, k_pos):
    rel = k_pos[None, :] - q_pos[:, None]
    bias = jnp.transpose(table[_t5_bucket(rel)].astype(F32), (2, 0, 1))
    far = table[N_BUCKETS // 2 - 1].astype(F32)[:, None, None]
    mask = (k_pos[None, :] // CHUNK) <= (q_pos[:, None] // CHUNK)
    return jnp.where(mask[None], bias - far, NEG_INF)


def _split_halves(q):
    lane = lax.broadcasted_iota(I32, q.shape, 1)
    zero = jnp.zeros_like(q)
    return jnp.concatenate(
        [jnp.where(lane < DK_DIFF, q, zero), jnp.where(lane >= DK_DIFF, q, zero)], axis=0)


def _softmax_step(qq, kj, vj, bias, m, l, acc):
    s = lax.dot_general(qq, kj, _NT, preferred_element_type=F32)
    if bias is not None:
        n2, tk = s.shape
        s = (s.reshape(2, n2 // 2, tk) + bias[None]).reshape(n2, tk)
    m_new = jnp.maximum(m, jnp.max(s, axis=1, keepdims=True))
    alpha = jnp.exp(m - m_new)
    p = jnp.exp(s - m_new)
    l = alpha * l + jnp.sum(p, axis=1, keepdims=True)
    acc = alpha * acc + jnp.dot(p.astype(BF16), vj, preferred_element_type=F32)
    return m_new, l, acc


def _diff_finish(l, acc, lam, gain, scale, tq):
    o = acc[:tq] / l[:tq] - lam * (acc[tq:] / l[tq:])
    o = o * lax.rsqrt(jnp.mean(o * o, axis=-1, keepdims=True) + RMS_EPS) * gain
    return (o * scale).astype(BF16)


def _log_keep(z):
    return -(jnp.maximum(z, 0.0) + jnp.log(1.0 + jnp.exp(-jnp.abs(z))))


def _sb_step(q, kj, vj, u, r, acc, mask):
    z = lax.dot_general(q, kj, _NT, preferred_element_type=F32)
    lk = _log_keep(z)
    if mask is not None:
        lk = jnp.where(mask, lk, 0.0)
    hi = lk.astype(BF16)
    lo = (lk - hi.astype(F32)).astype(BF16)
    s = (jnp.dot(hi, u, preferred_element_type=F32)
         + jnp.dot(lo, u, preferred_element_type=F32) + r)
    a = jnp.exp(z + s)
    if mask is not None:
        a = jnp.where(mask, a, 0.0)
    acc = acc + jnp.dot(a.astype(BF16), vj, preferred_element_type=F32)
    return s[:, 0:1], acc


def _sb_finish(acc, gain):
    o = acc * lax.rsqrt(jnp.mean(acc * acc, axis=-1, keepdims=True) + RMS_EPS) * gain
    return o.astype(BF16)


def _suffix_ones(n):
    r = lax.broadcasted_iota(I32, (n, n), 0)
    c = lax.broadcasted_iota(I32, (n, n), 1)
    return (r >= c).astype(BF16)


def _diff_prompt_kernel(lam_ref, q_ref, k_ref, v_ref, b0_ref, b1_ref, g_ref, o_ref, *, t, scale):
    i = pl.program_id(2)
    qq = _split_halves(q_ref[...])

    def kv(j):
        start = pl.multiple_of(j * t, t)
        return k_ref[pl.ds(start, t), :], v_ref[pl.ds(start, t), :]

    def far_body(j, carry):
        kj, vj = kv(j)
        return _softmax_step(qq, kj, vj, None, *carry)

    init = (jnp.full((2 * t, 1), NEG_INF, F32), jnp.zeros((2 * t, 1), F32),
            jnp.zeros((2 * t, HEAD_DIM), F32))
    carry = lax.fori_loop(0, jnp.maximum(i - 1, 0), far_body, init)

    def prev_block(c):
        kj, vj = kv(i - 1)
        return _softmax_step(qq, kj, vj, b1_ref[...], *c)

    carry = lax.cond(i >= 1, prev_block, lambda c: c, carry)
    kj, vj = kv(i)
    _, l, acc = _softmax_step(qq, kj, vj, b0_ref[...], *carry)
    o_ref[...] = _diff_finish(l, acc, lam_ref[0], g_ref[...], scale, t)


def _diff_prompt(lam, q, k, v, table, gain, scale, B, L):
    t = _pick_tile(L, (256, 128))
    nq = L // t
    pos = jnp.arange(t, dtype=I32)
    b0 = _bias_tile(table, pos + t, pos + t)
    b1 = _bias_tile(table, pos + t, pos)
    blk = pl.BlockSpec((t, HEAD_DIM), lambda b, h, i: (b * nq + i, h))
    seq = pl.BlockSpec((L, HEAD_DIM), lambda b, h, i: (b, h))
    bias = pl.BlockSpec((None, t, t), lambda b, h, i: (h, 0, 0))
    return pl.pallas_call(
        functools.partial(_diff_prompt_kernel, t=t, scale=scale),
        grid=(B, N_HEADS, nq),
        in_specs=[pl.BlockSpec(memory_space=pltpu.SMEM), blk, seq, seq, bias, bias,
                  pl.BlockSpec((1, HEAD_DIM), lambda b, h, i: (0, 0))],
        out_specs=blk,
        out_shape=jax.ShapeDtypeStruct((B * L, MIX_W), BF16),
        compiler_params=_cparams(3),
        name="diff_prompt",
    )(lam, q, k, v, b0, b1, gain)


def _sb_prompt_kernel(q_ref, k_ref, v_ref, g_ref, o_ref, *, t):
    i = pl.program_id(2)
    q = q_ref[...]
    u = _suffix_ones(t)
    row = lax.broadcasted_iota(I32, (t, t), 0)
    col = lax.broadcasted_iota(I32, (t, t), 1)

    def kv(j):
        start = pl.multiple_of(j * t, t)
        return k_ref[pl.ds(start, t), :], v_ref[pl.ds(start, t), :]

    kj, vj = kv(i)
    carry = _sb_step(q, kj, vj, u, jnp.zeros((t, 1), F32), jnp.zeros((t, HEAD_DIM), F32),
                     col < row)

    def body(n, c):
        kj, vj = kv(i - 1 - n)
        return _sb_step(q, kj, vj, u, c[0], c[1], None)

    _, acc = lax.fori_loop(0, i, body, carry)
    o_ref[...] = _sb_finish(acc, g_ref[...])


def _sb_prompt(q, k, v, gain, B, L):
    t = _pick_tile(L, (256, 128))
    nq = L // t
    blk = pl.BlockSpec((t, HEAD_DIM), lambda b, h, i: (b * nq + i, h))
    seq = pl.BlockSpec((L, HEAD_DIM), lambda b, h, i: (b, h))
    return pl.pallas_call(
        functools.partial(_sb_prompt_kernel, t=t),
        grid=(B, N_HEADS, nq),
        in_specs=[blk, seq, seq, pl.BlockSpec((1, HEAD_DIM), lambda b, h, i: (0, 0))],
        out_specs=blk,
        out_shape=jax.ShapeDtypeStruct((B * L, MIX_W), BF16),
        compiler_params=_cparams(3),
        name="sb_prompt",
    )(q, k, v, gain)


def _diff_sample_kernel(lam_ref, q_ref, kn_ref, vn_ref, kc_ref, vc_ref, bn_ref, bc_ref, g_ref,
                        o_ref, *, nb, scale):
    ls = q_ref.shape[0]
    p = kc_ref.shape[0]
    qq = _split_halves(q_ref[...])
    carry = (jnp.full((2 * ls, 1), NEG_INF, F32), jnp.zeros((2 * ls, 1), F32),
             jnp.zeros((2 * ls, HEAD_DIM), F32))
    if p > nb:
        carry = _softmax_step(qq, kc_ref[0:p - nb, :].astype(BF16),
                              vc_ref[0:p - nb, :].astype(BF16), None, *carry)
    carry = _softmax_step(qq, kc_ref[p - nb:p, :].astype(BF16),
                          vc_ref[p - nb:p, :].astype(BF16), bc_ref[...], *carry)
    _, l, acc = _softmax_step(qq, kn_ref[...], vn_ref[...], bn_ref[...], *carry)
    o_ref[...] = _diff_finish(l, acc, lam_ref[0], g_ref[...], scale, ls)


def _cache_spec(layer, p):
    return pl.BlockSpec((None, None, p, HEAD_DIM), lambda b, h: (layer, b, 0, h))


def _diff_sample(lam, q, k, v, cache_k, cache_v, layer, table, gain, scale, row0, Bs, Ls):
    p = cache_k.shape[2]
    nb = min(p, 128)
    assert nb >= MAX_DISTANCE or nb == p
    q_pos = p + jnp.arange(Ls, dtype=I32)
    bn = _bias_tile(table, q_pos, q_pos)
    bc = _bias_tile(table, q_pos, p - nb + jnp.arange(nb, dtype=I32))
    blk = pl.BlockSpec((Ls, HEAD_DIM), lambda b, h: (row0 // Ls + b, h))
    return pl.pallas_call(
        functools.partial(_diff_sample_kernel, nb=nb, scale=scale),
        grid=(Bs, N_HEADS),
        in_specs=[pl.BlockSpec(memory_space=pltpu.SMEM), blk, blk, blk,
                  _cache_spec(layer, p), _cache_spec(layer, p),
                  pl.BlockSpec((None, Ls, Ls), lambda b, h: (h, 0, 0)),
                  pl.BlockSpec((None, Ls, nb), lambda b, h: (h, 0, 0)),
                  pl.BlockSpec((1, HEAD_DIM), lambda b, h: (0, 0))],
        out_specs=pl.BlockSpec((Ls, HEAD_DIM), lambda b, h: (b, h)),
        out_shape=jax.ShapeDtypeStruct((Bs * Ls, MIX_W), BF16),
        compiler_params=_cparams(2),
        name="diff_sample",
    )(lam, q, k, v, cache_k, cache_v, bn, bc, gain)


def _sb_sample_kernel(q_ref, kn_ref, vn_ref, kc_ref, vc_ref, g_ref, o_ref, *, tc):
    ls = q_ref.shape[0]
    p = kc_ref.shape[0]
    q = q_ref[...]
    row = lax.broadcasted_iota(I32, (ls, ls), 0)
    col = lax.broadcasted_iota(I32, (ls, ls), 1)
    carry = _sb_step(q, kn_ref[...], vn_ref[...], _suffix_ones(ls), jnp.zeros((ls, 1), F32),
                     jnp.zeros((ls, HEAD_DIM), F32), col < row)
    u = _suffix_ones(tc)

    def body(n, c):
        start = pl.multiple_of(p - (n + 1) * tc, tc)
        kj = kc_ref[pl.ds(start, tc), :].astype(BF16)
        vj = vc_ref[pl.ds(start, tc), :].astype(BF16)
        return _sb_step(q, kj, vj, u, c[0], c[1], None)

    _, acc = lax.fori_loop(0, p // tc, body, carry)
    o_ref[...] = _sb_finish(acc, g_ref[...])


def _sb_sample(q, k, v, cache_k, cache_v, layer, gain, row0, Bs, Ls):
    p = cache_k.shape[2]
    tc = _pick_tile(p, (256, 128))
    blk = pl.BlockSpec((Ls, HEAD_DIM), lambda b, h: (row0 // Ls + b, h))
    return pl.pallas_call(
        functools.partial(_sb_sample_kernel, tc=tc),
        grid=(Bs, N_HEADS),
        in_specs=[blk, blk, blk, _cache_spec(layer, p), _cache_spec(layer, p),
                  pl.BlockSpec((1, HEAD_DIM), lambda b, h: (0, 0))],
        out_specs=pl.BlockSpec((Ls, HEAD_DIM), lambda b, h: (b, h)),
        out_shape=jax.ShapeDtypeStruct((Bs * Ls, MIX_W), BF16),
        compiler_params=_cparams(2),
        name="sb_sample",
    )(q, k, v, cache_k, cache_v, gain)


def _layer_norm(r, g, b):
    mu = jnp.mean(r, axis=-1, keepdims=True)
    c = r - mu
    var = jnp.mean(c * c, axis=-1, keepdims=True)
    return c * lax.rsqrt(var + LN_EPS) * g + b


def _outproj_kernel(od_ref, os_ref, x_ref, w_ref, g_ref, b_ref, h_ref, *, alpha):
    mix = (jnp.dot(od_ref[...], w_ref[0:MIX_W, :], preferred_element_type=F32)
           + jnp.dot(os_ref[...], w_ref[MIX_W:2 * MIX_W, :], preferred_element_type=F32))
    h_ref[...] = _layer_norm(alpha * x_ref[...] + mix, g_ref[...], b_ref[...])


def _out_projection(o_d, o_s, x_all, w_out_b, g, b, alpha):
    T, D = x_all.shape
    tm = _pick_tile(T, (512, 256, 128))
    row = lambda i: (i, 0)
    fixed = lambda i: (0, 0)
    return pl.pallas_call(
        functools.partial(_outproj_kernel, alpha=alpha),
        grid=(T // tm,),
        in_specs=[pl.BlockSpec((tm, MIX_W), row), pl.BlockSpec((tm, MIX_W), row),
                  pl.BlockSpec((tm, D), row), pl.BlockSpec(w_out_b.shape, fixed),
                  pl.BlockSpec((1, D), fixed), pl.BlockSpec((1, D), fixed)],
        out_specs=pl.BlockSpec((tm, D), row),
        out_shape=jax.ShapeDtypeStruct((T, D), F32),
        compiler_params=_cparams(1),
        name="out_projection_ln",
    )(o_d, o_s, x_all, w_out_b, g, b)


def _router_kernel(h_ref, wt_ref, rb_ref, us_ref, idx_ref, gate_ref, rank_ref, cnt_ref, base_sc):
    tm = h_ref.shape[0]

    @pl.when(pl.program_id(0) == 0)
    def _():
        base_sc[...] = jnp.zeros_like(base_sc)

    logits = lax.dot_general(wt_ref[...], h_ref[...], _NT, preferred_element_type=F32,
                             precision=lax.Precision.HIGHEST)
    scores = jax.nn.sigmoid(logits)
    sel = scores + rb_ref[...]

    sel3 = sel.reshape(N_GROUPS, GROUP_SIZE, tm)
    iw = lax.broadcasted_iota(I32, sel3.shape, 1)
    m1 = jnp.max(sel3, axis=1, keepdims=True)
    first = jnp.min(jnp.where(sel3 == m1, iw, GROUP_SIZE), axis=1, keepdims=True)
    m2 = jnp.max(jnp.where(iw == first, -jnp.inf, sel3), axis=1, keepdims=True)
    grp = m1 + m2

    ig = lax.broadcasted_iota(I32, grp.shape, 0)
    keep = jnp.zeros(grp.shape, F32)
    for _ in range(TOPK_GROUPS):
        mx = jnp.max(grp, axis=0, keepdims=True)
        pick = jnp.min(jnp.where(grp == mx, ig, N_GROUPS), axis=0, keepdims=True)
        hit = ig == pick
        keep = jnp.where(hit, 1.0, keep)
        grp = jnp.where(hit, -jnp.inf, grp)
    cand = jnp.where(keep > 0.5, sel3, -jnp.inf).reshape(N_EXPERTS, tm)

    ie = lax.broadcasted_iota(I32, cand.shape, 0)
    hits, gates = [], []
    for k in range(TOP_K):
        mx = jnp.max(cand, axis=0, keepdims=True)
        pick = jnp.min(jnp.where(cand == mx, ie, N_EXPERTS), axis=0, keepdims=True)
        hit = ie == pick
        idx_ref[k:k + 1, :] = pick
        gates.append(jnp.sum(jnp.where(hit, scores, 0.0), axis=0, keepdims=True))
        cand = jnp.where(hit, -jnp.inf, cand)
        hits.append(hit)
    g = jnp.concatenate(gates, axis=0)
    gate_ref[...] = g / (jnp.sum(g, axis=0, keepdims=True) + 1e-20) * ROUTED_SCALE

    base = base_sc[...]
    for k in range(TOP_K):
        oh = jnp.where(hits[k], 1.0, 0.0)
        before = jnp.dot(oh.astype(BF16), us_ref[...], preferred_element_type=F32)
        rank = jnp.sum(jnp.where(hits[k], base + before, 0.0), axis=0, keepdims=True)
        rank_ref[k:k + 1, :] = rank.astype(I32)
        base = base + jnp.sum(oh, axis=1, keepdims=True)
    base_sc[...] = base
    cnt_ref[...] = jnp.broadcast_to(base, cnt_ref.shape)


def _router(h, router_w, router_bias):
    T, D = h.shape
    tm = _pick_tile(T, (256, 128))
    r = lax.broadcasted_iota(I32, (tm, tm), 0)
    c = lax.broadcasted_iota(I32, (tm, tm), 1)
    strict_before = (r < c).astype(BF16)
    fixed = lambda i: (0, 0)
    col = lambda i: (0, i)
    return pl.pallas_call(
        _router_kernel,
        grid=(T // tm,),
        in_specs=[pl.BlockSpec((tm, D), lambda i: (i, 0)), pl.BlockSpec((N_EXPERTS, D), fixed),
                  pl.BlockSpec((N_EXPERTS, 1), fixed), pl.BlockSpec((tm, tm), fixed)],
        out_specs=[pl.BlockSpec((TOP_K, tm), col), pl.BlockSpec((TOP_K, tm), col),
                   pl.BlockSpec((TOP_K, tm), col), pl.BlockSpec((N_EXPERTS, 128), fixed)],
        out_shape=[jax.ShapeDtypeStruct((TOP_K, T), I32), jax.ShapeDtypeStruct((TOP_K, T), F32),
                   jax.ShapeDtypeStruct((TOP_K, T), I32),
                   jax.ShapeDtypeStruct((N_EXPERTS, 128), F32)],
        scratch_shapes=[pltpu.VMEM((N_EXPERTS, 1), F32)],
        compiler_params=_cparams(1),
        name="router_topk",
    )(h, router_w.T.astype(F32), router_bias.astype(F32).reshape(N_EXPERTS, 1), strict_before)


def _row_copy(src, src_row, dst, dst_row, sem):
    return pltpu.make_async_copy(src.at[pl.ds(src_row, 1), :], dst.at[pl.ds(dst_row, 1), :], sem)


def _dispatch_kernel(dest_ref, h_ref, buf_in_ref, buf_ref, sem):
    del buf_in_ref
    tm = h_ref.shape[0]

    def start_row(r, carry):
        for k in range(TOP_K):
            _row_copy(h_ref, r, buf_ref, dest_ref[k, r], sem).start()
        return carry

    lax.fori_loop(0, tm, start_row, 0)

    def wait_row(r, carry):
        for k in range(TOP_K):
            _row_copy(h_ref, r, buf_ref, dest_ref[k, r], sem).wait()
        return carry

    lax.fori_loop(0, tm, wait_row, 0)


def _dispatch(dest_tiles, h, n_rows):
    T, D = h.shape
    nt, _, tm = dest_tiles.shape
    buf0 = jnp.zeros((n_rows, D), F32)
    return pl.pallas_call(
        _dispatch_kernel,
        grid=(nt,),
        in_specs=[pl.BlockSpec((None, TOP_K, tm), lambda i: (i, 0, 0), memory_space=pltpu.SMEM),
                  pl.BlockSpec((tm, D), lambda i: (i, 0)),
                  pl.BlockSpec(memory_space=pl.ANY)],
        out_specs=pl.BlockSpec(memory_space=pl.ANY),
        out_shape=jax.ShapeDtypeStruct((n_rows, D), F32),
        scratch_shapes=[pltpu.SemaphoreType.DMA(())],
        input_output_aliases={2: 0},
        compiler_params=_cparams(1),
        name="moe_dispatch",
    )(dest_tiles, h, buf0)


def _expert_kernel(blk_e_ref, x_ref, wg_ref, wu_ref, wd_ref, y_ref):
    del blk_e_ref
    xb = x_ref[...].astype(BF16)
    hg = jnp.dot(xb, wg_ref[...], preferred_element_type=F32)
    hu = jnp.dot(xb, wu_ref[...], preferred_element_type=F32)
    a = (jax.nn.silu(hg) * hu).astype(BF16)
    y_ref[...] = jnp.dot(a, wd_ref[...], preferred_element_type=F32)


def _experts(blk_e, buf, wg, wu, wd):
    n_rows, D = buf.shape
    F = wg.shape[-1]
    grid_spec = pltpu.PrefetchScalarGridSpec(
        num_scalar_prefetch=1,
        grid=(n_rows // EBLK,),
        in_specs=[pl.BlockSpec((EBLK, D), lambda i, e: (i, 0)),
                  pl.BlockSpec((None, D, F), lambda i, e: (e[i], 0, 0)),
                  pl.BlockSpec((None, D, F), lambda i, e: (e[i], 0, 0)),
                  pl.BlockSpec((None, F, D), lambda i, e: (e[i], 0, 0))],
        out_specs=pl.BlockSpec((EBLK, D), lambda i, e: (i, 0)),
    )
    return pl.pallas_call(
        _expert_kernel,
        grid_spec=grid_spec,
        out_shape=jax.ShapeDtypeStruct((n_rows, D), F32),
        compiler_params=_cparams(1),
        name="moe_experts",
    )(blk_e, buf, wg, wu, wd)


def _combine_kernel(dest_ref, ybuf_ref, h_ref, gate_ref, sg_ref, su_ref, sd_ref, g_ref, b_ref,
                    y_ref, rows, sem, *, alpha):
    tm = h_ref.shape[0]

    def start_row(r, carry):
        for k in range(TOP_K):
            _row_copy(ybuf_ref, dest_ref[k, r], rows.at[k], r, sem).start()
        return carry

    lax.fori_loop(0, tm, start_row, 0)

    h = h_ref[...]
    hb = h.astype(BF16)
    hg = jnp.dot(hb, sg_ref[...], preferred_element_type=F32)
    hu = jnp.dot(hb, su_ref[...], preferred_element_type=F32)
    shared = jnp.dot((jax.nn.silu(hg) * hu).astype(BF16), sd_ref[...],
                     preferred_element_type=F32)

    def wait_row(r, carry):
        for k in range(TOP_K):
            _row_copy(ybuf_ref, dest_ref[k, r], rows.at[k], r, sem).wait()
        return carry

    lax.fori_loop(0, tm, wait_row, 0)

    routed = rows[0] * gate_ref[:, 0:1]
    for k in range(1, TOP_K):
        routed = routed + rows[k] * gate_ref[:, k:k + 1]
    y_ref[...] = _layer_norm(alpha * h + (routed + shared), g_ref[...], b_ref[...])


def _combine(dest_tiles, ybuf, h, gates_tk, sg, su, sd, g, b, alpha):
    T, D = h.shape
    nt, _, tm = dest_tiles.shape
    F = sg.shape[-1]
    row = lambda i: (i, 0)
    fixed = lambda i: (0, 0)
    return pl.pallas_call(
        functools.partial(_combine_kernel, alpha=alpha),
        grid=(nt,),
        in_specs=[pl.BlockSpec((None, TOP_K, tm), lambda i: (i, 0, 0), memory_space=pltpu.SMEM),
                  pl.BlockSpec(memory_space=pl.ANY),
                  pl.BlockSpec((tm, D), row), pl.BlockSpec((tm, TOP_K), row),
                  pl.BlockSpec((D, F), fixed), pl.BlockSpec((D, F), fixed),
                  pl.BlockSpec((F, D), fixed), pl.BlockSpec((1, D), fixed),
                  pl.BlockSpec((1, D), fixed)],
        out_specs=pl.BlockSpec((tm, D), row),
        out_shape=jax.ShapeDtypeStruct((T, D), F32),
        scratch_shapes=[pltpu.VMEM((TOP_K, tm, D), F32), pltpu.SemaphoreType.DMA(())],
        compiler_params=_cparams(1),
        name="moe_combine",
    )(dest_tiles, ybuf, h, gates_tk, sg, su, sd, g, b)


def _moe(h, router_w, router_bias, e_gate, e_up, e_down, s_gate, s_up, s_down, g, b, alpha):
    T, D = h.shape
    idx, gates, rank, counts = _router(h, router_w, router_bias)
    tm = _pick_tile(T, (256, 128))
    nt = T // tm
    sizes = counts[:, 0].astype(I32)
    padded = (sizes + EBLK - 1) // EBLK * EBLK
    pend = jnp.cumsum(padded)
    dest = (pend - padded)[idx] + rank
    n_blk = (T * TOP_K + N_EXPERTS * (EBLK - 1) + EBLK - 1) // EBLK
    blk_e = jnp.minimum(
        jnp.searchsorted(pend, jnp.arange(n_blk, dtype=I32) * EBLK, side="right"),
        N_EXPERTS - 1).astype(I32)
    dest_tiles = dest.reshape(TOP_K, nt, tm).transpose(1, 0, 2)
    buf = _dispatch(dest_tiles, h, n_blk * EBLK)
    ybuf = _experts(blk_e, buf, e_gate.astype(BF16), e_up.astype(BF16), e_down.astype(BF16))
    return _combine(dest_tiles, ybuf, h, gates.T, s_gate.astype(BF16), s_up.astype(BF16),
                    s_down.astype(BF16), g, b, alpha)


def _layer(layer, depth, x_all, B, L, Bs, Ls, caches, table, w_in, lq1, lk1, lq2, lk2, subln_w,
           sb_w, w_out, ln1_g, ln1_b, router_w, router_bias, e_gate, e_up, e_down, s_gate, s_up,
           s_down, ln2_g, ln2_b):
    alpha = (2.0 * depth) ** 0.25
    lam_init = 0.8 - 0.6 * math.exp(-0.3 * layer)
    Tp = B * L
    D = x_all.shape[1]

    qd, kd, vd, qs, ks, vs, kdb, vdb, ksb, vsb = _in_projection(x_all, w_in.astype(BF16))

    lam = (jnp.exp(jnp.sum(lq1.astype(F32) * lk1.astype(F32)))
           - jnp.exp(jnp.sum(lq2.astype(F32) * lk2.astype(F32))) + lam_init).reshape(1)
    gain_d = subln_w.astype(F32).reshape(1, HEAD_DIM)
    gain_s = sb_w.astype(F32).reshape(1, HEAD_DIM)
    scale_d = 1.0 - lam_init
    ck, cv, csk, csv = caches

    od = jnp.concatenate([
        _diff_prompt(lam, qd, kdb, vdb, table, gain_d, scale_d, B, L),
        _diff_sample(lam, qd, kdb, vdb, ck, cv, layer, table, gain_d, scale_d, Tp, Bs, Ls)])
    osb = jnp.concatenate([
        _sb_prompt(qs, ksb, vsb, gain_s, B, L),
        _sb_sample(qs, ksb, vsb, csk, csv, layer, gain_s, Tp, Bs, Ls)])

    row = lambda v: v.astype(F32).reshape(1, D)
    h = _out_projection(od, osb, x_all, w_out.astype(BF16), row(ln1_g), row(ln1_b), alpha)
    y = _moe(h, router_w, router_bias, e_gate, e_up, e_down, s_gate, s_up, s_down,
             row(ln2_g), row(ln2_b), alpha)
    return y, (kd, vd, ks, vs)


def kernel(x_prompt, x_sample, cache_diff_k, cache_diff_v, cache_sb_k, cache_sb_v, rel_bias_table, w_in, diff_lambda_q1, diff_lambda_k1, diff_lambda_q2, diff_lambda_k2, diff_subln_w, sb_norm_w, w_out, ln1_g, ln1_b, router_w, router_bias, expert_w_gate, expert_w_up, expert_w_down, shared_w_gate, shared_w_up, shared_w_down, ln2_g, ln2_b):
    B, L, D = x_prompt.shape
    Bs, Ls, _ = x_sample.shape
    depth = w_in.shape[0]
    Tp, Ts = B * L, Bs * Ls
    assert Tp % Ls == 0
    x_all = jnp.concatenate([x_prompt.reshape(Tp, D), x_sample.reshape(Ts, D)])
    caches = tuple(c.reshape(*c.shape[:3], MIX_W)
                   for c in (cache_diff_k, cache_diff_v, cache_sb_k, cache_sb_v))
    rows = [[] for _ in range(8)]
    for l in range(depth):
        x_all, new = _layer(
            l, depth, x_all, B, L, Bs, Ls, caches, rel_bias_table, w_in[l], diff_lambda_q1[l],
            diff_lambda_k1[l], diff_lambda_q2[l], diff_lambda_k2[l], diff_subln_w[l],
            sb_norm_w[l], w_out[l], ln1_g[l], ln1_b[l], router_w[l], router_bias[l],
            expert_w_gate[l], expert_w_up[l], expert_w_down[l], shared_w_gate[l],
            shared_w_up[l], shared_w_down[l], ln2_g[l], ln2_b[l])
        for n, a in enumerate(new):
            rows[n].append(a[:Tp].reshape(B, L, N_HEADS, HEAD_DIM))
            rows[4 + n].append(a[Tp:].reshape(Bs, Ls, N_HEADS, HEAD_DIM))
    return (x_all[:Tp].reshape(B, L, D), x_all[Tp:].reshape(Bs, Ls, D),
            *[jnp.stack(r) for r in rows])
```

```python
import functools
import math

import jax
import jax.numpy as jnp
from jax import lax
from jax.experimental import pallas as pl
from jax.experimental.pallas import tpu as pltpu

F32 = jnp.float32
BF16 = jnp.bfloat16
I32 = jnp.int32

HEAD_DIM = 128
N_HEADS = 4
DK_DIFF = HEAD_DIM // 2
MIX_W = N_HEADS * HEAD_DIM
CHUNK = 64
N_BUCKETS = 32
MAX_DISTANCE = 128
N_EXPERTS = 64
N_GROUPS = 8
GROUP_SIZE = N_EXPERTS // N_GROUPS
TOPK_GROUPS = 4
TOP_K = 6
EBLK = 256
ROUTED_SCALE = 2.5
LN_EPS = 1e-5
RMS_EPS = 1e-5
NEG_INF = -1e30
LOG2E = 1.4426950408889634

VMEM_LIMIT_BYTES = 56 * 1024 * 1024
SB_HEADS_PER_STEP = 2
ATT_TILE = 512
SB_SUB = 256

_NT = (((1,), (1,)), ((), ()))


def _cparams(n_axes):
    return pltpu.CompilerParams(
        dimension_semantics=("arbitrary",) * n_axes, vmem_limit_bytes=VMEM_LIMIT_BYTES)


def _pick_tile(n, candidates):
    for c in candidates:
        if n % c == 0:
            return c
    raise ValueError(f"no tile in {candidates} divides {n}")


def _inproj_kernel(x_ref, w_ref, qd_ref, kd_ref, vd_ref, qs_ref, ks_ref, vs_ref,
                   kdb_ref, vdb_ref, ksb_ref, vsb_ref):
    xb = x_ref[...].astype(BF16)

    def proj(c):
        return jnp.dot(xb, w_ref[:, c * MIX_W:(c + 1) * MIX_W], preferred_element_type=F32)

    qd_ref[...] = (proj(0) * (DK_DIFF ** -0.5 * LOG2E)).astype(BF16)
    kd = proj(1)
    kd_ref[...] = kd
    kdb_ref[...] = kd.astype(BF16)
    vd = proj(2)
    vd_ref[...] = vd
    vdb_ref[...] = vd.astype(BF16)
    qs_ref[...] = (proj(3) * (HEAD_DIM ** -0.5 * LOG2E)).astype(BF16)
    ks = proj(4)
    ks_ref[...] = ks
    ksb_ref[...] = ks.astype(BF16)
    vs = proj(5)
    vs_ref[...] = vs
    vsb_ref[...] = vs.astype(BF16)


def _in_projection(x_all, w_in_b):
    T, D = x_all.shape
    tm = _pick_tile(T, (512, 256, 128))
    row = lambda i: (i, 0)
    o_spec = pl.BlockSpec((tm, MIX_W), row)
    f32_o = jax.ShapeDtypeStruct((T, MIX_W), F32)
    b16_o = jax.ShapeDtypeStruct((T, MIX_W), BF16)
    return pl.pallas_call(
        _inproj_kernel,
        grid=(T // tm,),
        in_specs=[pl.BlockSpec((tm, D), row), pl.BlockSpec(w_in_b.shape, lambda i: (0, 0))],
        out_specs=[o_spec] * 10,
        out_shape=[b16_o, f32_o, f32_o, b16_o, f32_o, f32_o, b16_o, b16_o, b16_o, b16_o],
        compiler_params=_cparams(1),
        name="in_projection",
    )(x_all, w_in_b)


def _t5_bucket(rel):
    nb = N_BUCKETS // 2
    ret = jnp.where(rel > 0, nb, 0)
    n = jnp.abs(rel)
    max_exact = nb // 2
    nf = jnp.maximum(n, 1).astype(F32)
    large = max_exact + (jnp.log(nf / max_exact) / math.log(MAX_DISTANCE / max_exact)
                         * (nb - max_exact)).astype(I32)
    large = jnp.minimum(large, nb - 1)
    return ret + jnp.where(n < max_exact, n, large)


def _bias_tile(table, q_pos, k_pos):
    rel = k_pos[None, :] - q_pos[:, None]
    bucket = _t5_bucket(rel)
    tab = table.astype(F32)
    bias = sum(jnp.where(bucket[None] == n, tab[n][:, None, None], 0.0) for n in range(N_BUCKETS))
    far = tab[N_BUCKETS // 2 - 1][:, None, None]
    mask = (k_pos[None, :] // CHUNK) <= (q_pos[:, None] // CHUNK)
    return jnp.where(mask[None], (bias - far) * LOG2E, NEG_INF)


def _split_halves(q):
    lane = lax.broadcasted_iota(I32, q.shape, 1)
    zero = jnp.zeros_like(q)
    return jnp.concatenate(
        [jnp.where(lane < DK_DIFF, q, zero), jnp.where(lane >= DK_DIFF, q, zero)], axis=0)


def _softmax_step(qq, kj, vj, bias, m, l, acc):
    s = lax.dot_general(qq, kj, _NT, preferred_element_type=F32)
    if bias is not None:
        n2, tk = s.shape
        s = (s.reshape(2, n2 // 2, tk) + bias[None]).reshape(n2, tk)
    m_new = jnp.maximum(m, jnp.max(s, axis=1, keepdims=True))
    alpha = jnp.exp2(m - m_new)
    p = jnp.exp2(s - m_new)
    l = alpha * l + jnp.sum(p, axis=1, keepdims=True)
    acc = alpha * acc + jnp.dot(p.astype(BF16), vj, preferred_element_type=F32)
    return m_new, l, acc


def _diff_finish(l, acc, lam, gain, scale, tq):
    o = acc[:tq] / l[:tq] - lam * (acc[tq:] / l[tq:])
    o = o * lax.rsqrt(jnp.mean(o * o, axis=-1, keepdims=True) + RMS_EPS) * gain
    return (o * scale).astype(BF16)


def _softplus2(z):
    return jnp.maximum(z, 0.0) + jnp.log(1.0 + jnp.exp2(-jnp.abs(z))) * LOG2E


def _sb_step(q, kj, vj, u, r, acc, mask):
    z = lax.dot_general(q, kj, _NT, preferred_element_type=F32)
    sp = _softplus2(z)
    if mask is not None:
        sp = jnp.where(mask, sp, 0.0)
    sp = sp.astype(BF16)
    sub = u.shape[0]
    parts = []
    for c in reversed(range(kj.shape[0] // sub)):
        s_c = jnp.dot(sp[:, c * sub:(c + 1) * sub], u, preferred_element_type=F32) + r
        r = s_c[:, 0:1]
        parts.insert(0, s_c)
    s = parts[0] if len(parts) == 1 else jnp.concatenate(parts, axis=1)
    a = jnp.exp2(z - s)
    if mask is not None:
        a = jnp.where(mask, a, 0.0)
    acc = acc + jnp.dot(a.astype(BF16), vj, preferred_element_type=F32)
    return r, acc


def _sb_finish(acc, gain):
    o = acc * lax.rsqrt(jnp.mean(acc * acc, axis=-1, keepdims=True) + RMS_EPS) * gain
    return o.astype(BF16)


def _suffix_ones(n):
    r = lax.broadcasted_iota(I32, (n, n), 0)
    c = lax.broadcasted_iota(I32, (n, n), 1)
    return (r >= c).astype(BF16)


def _diff_prompt_kernel(lam_ref, q_ref, k_ref, v_ref, b0_ref, b1_ref, g_ref, o_ref, *, t, scale):
    i = pl.program_id(2)
    qq = _split_halves(q_ref[...])

    def kv(j):
        start = pl.multiple_of(j * t, t)
        return k_ref[pl.ds(start, t), :], v_ref[pl.ds(start, t), :]

    def far_body(j, carry):
        kj, vj = kv(j)
        return _softmax_step(qq, kj, vj, None, *carry)

    init = (jnp.full((2 * t, 1), NEG_INF, F32), jnp.zeros((2 * t, 1), F32),
            jnp.zeros((2 * t, HEAD_DIM), F32))
    carry = lax.fori_loop(0, jnp.maximum(i - 1, 0), far_body, init)

    def prev_block(c):
        kj, vj = kv(i - 1)
        return _softmax_step(qq, kj, vj, b1_ref[...], *c)

    carry = lax.cond(i >= 1, prev_block, lambda c: c, carry)
    kj, vj = kv(i)
    _, l, acc = _softmax_step(qq, kj, vj, b0_ref[...], *carry)
    o_ref[...] = _diff_finish(l, acc, lam_ref[0], g_ref[...], scale, t)


def _diff_prompt(lam, q, k, v, table, gain, scale, B, L):
    t = _pick_tile(L, (ATT_TILE, 256, 128))
    nq = L // t
    pos = jnp.arange(t, dtype=I32)
    b0 = _bias_tile(table, pos + t, pos + t)
    b1 = _bias_tile(table, pos + t, pos)
    blk = pl.BlockSpec((t, HEAD_DIM), lambda b, h, i: (b * nq + i, h))
    seq = pl.BlockSpec((L, HEAD_DIM), lambda b, h, i: (b, h))
    bias = pl.BlockSpec((None, t, t), lambda b, h, i: (h, 0, 0))
    return pl.pallas_call(
        functools.partial(_diff_prompt_kernel, t=t, scale=scale),
        grid=(B, N_HEADS, nq),
        in_specs=[pl.BlockSpec(memory_space=pltpu.SMEM), blk, seq, seq, bias, bias,
                  pl.BlockSpec((1, HEAD_DIM), lambda b, h, i: (0, 0))],
        out_specs=blk,
        out_shape=jax.ShapeDtypeStruct((B * L, MIX_W), BF16),
        compiler_params=_cparams(3),
        name="diff_prompt",
    )(lam, q, k, v, b0, b1, gain)


def _sb_prompt_kernel(q_ref, k_ref, v_ref, g_ref, o_ref, *, t, nh):
    i = pl.program_id(2)
    u = _suffix_ones(min(t, SB_SUB))
    row = lax.broadcasted_iota(I32, (t, t), 0)
    col = lax.broadcasted_iota(I32, (t, t), 1)
    lanes = [slice(n * HEAD_DIM, (n + 1) * HEAD_DIM) for n in range(nh)]
    qs = [q_ref[:, ln] for ln in lanes]

    def step(j, carry, mask):
        start = pl.multiple_of(j * t, t)
        out = []
        for n, ln in enumerate(lanes):
            out.extend(_sb_step(qs[n], k_ref[pl.ds(start, t), ln], v_ref[pl.ds(start, t), ln], u,
                                carry[2 * n], carry[2 * n + 1], mask))
        return tuple(out)

    init = (jnp.zeros((t, 1), F32), jnp.zeros((t, HEAD_DIM), F32)) * nh
    carry = step(i, init, col < row)
    carry = lax.fori_loop(0, i, lambda n, c: step(i - 1 - n, c, None), carry)
    for n, ln in enumerate(lanes):
        o_ref[:, ln] = _sb_finish(carry[2 * n + 1], g_ref[...])


def _sb_prompt(q, k, v, gain, B, L):
    t = _pick_tile(L, (ATT_TILE, 256, 128))
    nq = L // t
    nh = SB_HEADS_PER_STEP
    w = nh * HEAD_DIM
    blk = pl.BlockSpec((t, w), lambda b, h, i: (b * nq + i, h))
    seq = pl.BlockSpec((L, w), lambda b, h, i: (b, h))
    return pl.pallas_call(
        functools.partial(_sb_prompt_kernel, t=t, nh=nh),
        grid=(B, N_HEADS // nh, nq),
        in_specs=[blk, seq, seq, pl.BlockSpec((1, HEAD_DIM), lambda b, h, i: (0, 0))],
        out_specs=blk,
        out_shape=jax.ShapeDtypeStruct((B * L, MIX_W), BF16),
        compiler_params=_cparams(3),
        name="sb_prompt",
    )(q, k, v, gain)


def _cache_rows(ref, h, start, n):
    return ref[pl.ds(start * N_HEADS + h, n, stride=N_HEADS), :].astype(BF16)


def _cache_spec(layer, n_batch, p):
    return pl.BlockSpec((p * N_HEADS, HEAD_DIM), lambda b: (layer * n_batch + b, 0))


def _head_lanes(h):
    return slice(h * HEAD_DIM, (h + 1) * HEAD_DIM)


def _diff_sample_kernel(lam_ref, q_ref, kn_ref, vn_ref, kc_ref, vc_ref, bn_ref, bc_ref, g_ref,
                        o_ref, *, p, nb, scale):
    ls = q_ref.shape[0]
    for h in range(N_HEADS):
        ln = _head_lanes(h)
        qq = _split_halves(q_ref[:, ln])
        carry = (jnp.full((2 * ls, 1), NEG_INF, F32), jnp.zeros((2 * ls, 1), F32),
                 jnp.zeros((2 * ls, HEAD_DIM), F32))
        if p > nb:
            carry = _softmax_step(qq, _cache_rows(kc_ref, h, 0, p - nb),
                                  _cache_rows(vc_ref, h, 0, p - nb), None, *carry)
        carry = _softmax_step(qq, _cache_rows(kc_ref, h, p - nb, nb),
                              _cache_rows(vc_ref, h, p - nb, nb), bc_ref[h], *carry)
        _, l, acc = _softmax_step(qq, kn_ref[:, ln], vn_ref[:, ln], bn_ref[h], *carry)
        o_ref[:, ln] = _diff_finish(l, acc, lam_ref[0], g_ref[...], scale, ls)


def _diff_sample(lam, q, k, v, cache_k, cache_v, p, layer, table, gain, scale, row0, Bs, Ls):
    nb = min(p, MAX_DISTANCE)
    q_pos = p + jnp.arange(Ls, dtype=I32)
    bn = _bias_tile(table, q_pos, q_pos)
    bc = _bias_tile(table, q_pos, p - nb + jnp.arange(nb, dtype=I32))
    blk = pl.BlockSpec((Ls, MIX_W), lambda b: (row0 // Ls + b, 0))
    fixed3 = lambda b: (0, 0, 0)
    return pl.pallas_call(
        functools.partial(_diff_sample_kernel, p=p, nb=nb, scale=scale),
        grid=(Bs,),
        in_specs=[pl.BlockSpec(memory_space=pltpu.SMEM), blk, blk, blk,
                  _cache_spec(layer, Bs, p), _cache_spec(layer, Bs, p),
                  pl.BlockSpec(bn.shape, fixed3), pl.BlockSpec(bc.shape, fixed3),
                  pl.BlockSpec((1, HEAD_DIM), lambda b: (0, 0))],
        out_specs=pl.BlockSpec((Ls, MIX_W), lambda b: (b, 0)),
        out_shape=jax.ShapeDtypeStruct((Bs * Ls, MIX_W), BF16),
        compiler_params=_cparams(1),
        name="diff_sample",
    )(lam, q, k, v, cache_k, cache_v, bn, bc, gain)


def _sb_sample_kernel(q_ref, kn_ref, vn_ref, kc_ref, vc_ref, g_ref, o_ref, *, p, tc):
    ls = q_ref.shape[0]
    row = lax.broadcasted_iota(I32, (ls, ls), 0)
    col = lax.broadcasted_iota(I32, (ls, ls), 1)
    u_new = _suffix_ones(ls)
    u = _suffix_ones(min(tc, SB_SUB))
    qs = [q_ref[:, _head_lanes(h)] for h in range(N_HEADS)]
    carry = []
    for h in range(N_HEADS):
        ln = _head_lanes(h)
        carry.extend(_sb_step(qs[h], kn_ref[:, ln], vn_ref[:, ln], u_new, jnp.zeros((ls, 1), F32),
                              jnp.zeros((ls, HEAD_DIM), F32), col < row))

    def body(n, c):
        start = p - (n + 1) * tc
        out = []
        for h in range(N_HEADS):
            out.extend(_sb_step(qs[h], _cache_rows(kc_ref, h, start, tc),
                                _cache_rows(vc_ref, h, start, tc), u, c[2 * h], c[2 * h + 1],
                                None))
        return tuple(out)

    carry = lax.fori_loop(0, p // tc, body, tuple(carry))
    for h in range(N_HEADS):
        o_ref[:, _head_lanes(h)] = _sb_finish(carry[2 * h + 1], g_ref[...])


def _sb_sample(q, k, v, cache_k, cache_v, p, layer, gain, row0, Bs, Ls):
    tc = _pick_tile(p, (512, 256, 128))
    blk = pl.BlockSpec((Ls, MIX_W), lambda b: (row0 // Ls + b, 0))
    return pl.pallas_call(
        functools.partial(_sb_sample_kernel, p=p, tc=tc),
        grid=(Bs,),
        in_specs=[blk, blk, blk, _cache_spec(layer, Bs, p), _cache_spec(layer, Bs, p),
                  pl.BlockSpec((1, HEAD_DIM), lambda b: (0, 0))],
        out_specs=pl.BlockSpec((Ls, MIX_W), lambda b: (b, 0)),
        out_shape=jax.ShapeDtypeStruct((Bs * Ls, MIX_W), BF16),
        compiler_params=_cparams(1),
        name="sb_sample",
    )(q, k, v, cache_k, cache_v, gain)


def _layer_norm(r, g, b):
    mu = jnp.mean(r, axis=-1, keepdims=True)
    c = r - mu
    var = jnp.mean(c * c, axis=-1, keepdims=True)
    return c * lax.rsqrt(var + LN_EPS) * g + b


def _outproj_kernel(od_ref, os_ref, x_ref, w_ref, g_ref, b_ref, h_ref, *, alpha):
    mix = (jnp.dot(od_ref[...], w_ref[0:MIX_W, :], preferred_element_type=F32)
           + jnp.dot(os_ref[...], w_ref[MIX_W:2 * MIX_W, :], preferred_element_type=F32))
    h_ref[...] = _layer_norm(alpha * x_ref[...] + mix, g_ref[...], b_ref[...])


def _out_projection(o_d, o_s, x_all, w_out_b, g, b, alpha):
    T, D = x_all.shape
    tm = _pick_tile(T, (512, 256, 128))
    row = lambda i: (i, 0)
    fixed = lambda i: (0, 0)
    return pl.pallas_call(
        functools.partial(_outproj_kernel, alpha=alpha),
        grid=(T // tm,),
        in_specs=[pl.BlockSpec((tm, MIX_W), row), pl.BlockSpec((tm, MIX_W), row),
                  pl.BlockSpec((tm, D), row), pl.BlockSpec(w_out_b.shape, fixed),
                  pl.BlockSpec((1, D), fixed), pl.BlockSpec((1, D), fixed)],
        out_specs=pl.BlockSpec((tm, D), row),
        out_shape=jax.ShapeDtypeStruct((T, D), F32),
        compiler_params=_cparams(1),
        name="out_projection_ln",
    )(o_d, o_s, x_all, w_out_b, g, b)


def _router_kernel(h_ref, wt_ref, rb_ref, us_ref, idx_ref, gate_ref, rank_ref, cnt_ref, base_sc):
    tm = h_ref.shape[0]

    @pl.when(pl.program_id(0) == 0)
    def _():
        base_sc[...] = jnp.zeros_like(base_sc)

    logits = lax.dot_general(wt_ref[...], h_ref[...], _NT, preferred_element_type=F32,
                             precision=lax.Precision.HIGHEST)
    scores = jax.nn.sigmoid(logits)
    sel = scores + rb_ref[...]

    sel3 = sel.reshape(N_GROUPS, GROUP_SIZE, tm)
    iw = lax.broadcasted_iota(I32, sel3.shape, 1)
    m1 = jnp.max(sel3, axis=1, keepdims=True)
    first = jnp.min(jnp.where(sel3 == m1, iw, GROUP_SIZE), axis=1, keepdims=True)
    m2 = jnp.max(jnp.where(iw == first, -jnp.inf, sel3), axis=1, keepdims=True)
    grp = m1 + m2

    ig = lax.broadcasted_iota(I32, grp.shape, 0)
    keep = jnp.zeros(grp.shape, F32)
    for _ in range(TOPK_GROUPS):
        mx = jnp.max(grp, axis=0, keepdims=True)
        pick = jnp.min(jnp.where(grp == mx, ig, N_GROUPS), axis=0, keepdims=True)
        hit = ig == pick
        keep = jnp.where(hit, 1.0, keep)
        grp = jnp.where(hit, -jnp.inf, grp)
    cand = jnp.where(keep > 0.5, sel3, -jnp.inf).reshape(N_EXPERTS, tm)

    ie = lax.broadcasted_iota(I32, cand.shape, 0)
    hits, gates = [], []
    for k in range(TOP_K):
        mx = jnp.max(cand, axis=0, keepdims=True)
        pick = jnp.min(jnp.where(cand == mx, ie, N_EXPERTS), axis=0, keepdims=True)
        hit = ie == pick
        idx_ref[k:k + 1, :] = pick
        gates.append(jnp.sum(jnp.where(hit, scores, 0.0), axis=0, keepdims=True))
        cand = jnp.where(hit, -jnp.inf, cand)
        hits.append(hit)
    g = jnp.concatenate(gates, axis=0)
    gate_ref[...] = g / (jnp.sum(g, axis=0, keepdims=True) + 1e-20) * ROUTED_SCALE

    base = base_sc[...]
    for k in range(TOP_K):
        oh = jnp.where(hits[k], 1.0, 0.0)
        before = jnp.dot(oh.astype(BF16), us_ref[...], preferred_element_type=F32)
        rank = jnp.sum(jnp.where(hits[k], base + before, 0.0), axis=0, keepdims=True)
        rank_ref[k:k + 1, :] = rank.astype(I32)
        base = base + jnp.sum(oh, axis=1, keepdims=True)
    base_sc[...] = base
    cnt_ref[...] = jnp.broadcast_to(base, cnt_ref.shape)


def _router(h, router_w, router_bias):
    T, D = h.shape
    tm = _pick_tile(T, (256, 128))
    r = lax.broadcasted_iota(I32, (tm, tm), 0)
    c = lax.broadcasted_iota(I32, (tm, tm), 1)
    strict_before = (r < c).astype(BF16)
    fixed = lambda i: (0, 0)
    col = lambda i: (0, i)
    return pl.pallas_call(
        _router_kernel,
        grid=(T // tm,),
        in_specs=[pl.BlockSpec((tm, D), lambda i: (i, 0)), pl.BlockSpec((N_EXPERTS, D), fixed),
                  pl.BlockSpec((N_EXPERTS, 1), fixed), pl.BlockSpec((tm, tm), fixed)],
        out_specs=[pl.BlockSpec((TOP_K, tm), col), pl.BlockSpec((TOP_K, tm), col),
                   pl.BlockSpec((TOP_K, tm), col), pl.BlockSpec((N_EXPERTS, 128), fixed)],
        out_shape=[jax.ShapeDtypeStruct((TOP_K, T), I32), jax.ShapeDtypeStruct((TOP_K, T), F32),
                   jax.ShapeDtypeStruct((TOP_K, T), I32),
                   jax.ShapeDtypeStruct((N_EXPERTS, 128), F32)],
        scratch_shapes=[pltpu.VMEM((N_EXPERTS, 1), F32)],
        compiler_params=_cparams(1),
        name="router_topk",
    )(h, router_w.T.astype(F32), router_bias.astype(F32).reshape(N_EXPERTS, 1), strict_before)


def _row_copy(src, src_row, dst, dst_row, sem):
    return pltpu.make_async_copy(src.at[pl.ds(src_row, 1), :], dst.at[pl.ds(dst_row, 1), :], sem)


def _dispatch_kernel(dest_ref, h_ref, buf_in_ref, buf_ref, sem):
    del buf_in_ref
    tm = h_ref.shape[0]

    def start_row(r, carry):
        for k in range(TOP_K):
            _row_copy(h_ref, r, buf_ref, dest_ref[k, r], sem).start()
        return carry

    lax.fori_loop(0, tm, start_row, 0, unroll=4)

    def wait_row(r, carry):
        for k in range(TOP_K):
            _row_copy(h_ref, r, buf_ref, dest_ref[k, r], sem).wait()
        return carry

    lax.fori_loop(0, tm, wait_row, 0)


def _dispatch(dest_tiles, h, n_rows):
    T, D = h.shape
    nt, _, tm = dest_tiles.shape
    buf0 = jnp.zeros((n_rows, D), F32)
    return pl.pallas_call(
        _dispatch_kernel,
        grid=(nt,),
        in_specs=[pl.BlockSpec((None, TOP_K, tm), lambda i: (i, 0, 0), memory_space=pltpu.SMEM),
                  pl.BlockSpec((tm, D), lambda i: (i, 0)),
                  pl.BlockSpec(memory_space=pl.ANY)],
        out_specs=pl.BlockSpec(memory_space=pl.ANY),
        out_shape=jax.ShapeDtypeStruct((n_rows, D), F32),
        scratch_shapes=[pltpu.SemaphoreType.DMA(())],
        input_output_aliases={2: 0},
        compiler_params=_cparams(1),
        name="moe_dispatch",
    )(dest_tiles, h, buf0)


def _expert_kernel(blk_e_ref, x_ref, wg_ref, wu_ref, wd_ref, y_ref):
    del blk_e_ref
    xb = x_ref[...].astype(BF16)
    hg = jnp.dot(xb, wg_ref[...], preferred_element_type=F32)
    hu = jnp.dot(xb, wu_ref[...], preferred_element_type=F32)
    a = (jax.nn.silu(hg) * hu).astype(BF16)
    y_ref[...] = jnp.dot(a, wd_ref[...], preferred_element_type=F32)


def _experts(blk_e, buf, wg, wu, wd):
    n_rows, D = buf.shape
    F = wg.shape[-1]
    grid_spec = pltpu.PrefetchScalarGridSpec(
        num_scalar_prefetch=1,
        grid=(n_rows // EBLK,),
        in_specs=[pl.BlockSpec((EBLK, D), lambda i, e: (i, 0)),
                  pl.BlockSpec((None, D, F), lambda i, e: (e[i], 0, 0)),
                  pl.BlockSpec((None, D, F), lambda i, e: (e[i], 0, 0)),
                  pl.BlockSpec((None, F, D), lambda i, e: (e[i], 0, 0))],
        out_specs=pl.BlockSpec((EBLK, D), lambda i, e: (i, 0)),
    )
    return pl.pallas_call(
        _expert_kernel,
        grid_spec=grid_spec,
        out_shape=jax.ShapeDtypeStruct((n_rows, D), F32),
        compiler_params=_cparams(1),
        name="moe_experts",
    )(blk_e, buf, wg, wu, wd)


def _combine_kernel(dest_ref, ybuf_ref, h_ref, gate_ref, sg_ref, su_ref, sd_ref, g_ref, b_ref,
                    y_ref, rows, sem, *, alpha):
    tm = h_ref.shape[0]

    def start_row(r, carry):
        for k in range(TOP_K):
            _row_copy(ybuf_ref, dest_ref[k, r], rows.at[k], r, sem).start()
        return carry

    lax.fori_loop(0, tm, start_row, 0, unroll=4)

    h = h_ref[...]
    hb = h.astype(BF16)
    hg = jnp.dot(hb, sg_ref[...], preferred_element_type=F32)
    hu = jnp.dot(hb, su_ref[...], preferred_element_type=F32)
    shared = jnp.dot((jax.nn.silu(hg) * hu).astype(BF16), sd_ref[...],
                     preferred_element_type=F32)

    def wait_row(r, carry):
        for k in range(TOP_K):
            _row_copy(ybuf_ref, dest_ref[k, r], rows.at[k], r, sem).wait()
        return carry

    lax.fori_loop(0, tm, wait_row, 0)

    routed = rows[0] * gate_ref[:, 0:1]
    for k in range(1, TOP_K):
        routed = routed + rows[k] * gate_ref[:, k:k + 1]
    y_ref[...] = _layer_norm(alpha * h + (routed + shared), g_ref[...], b_ref[...])


def _combine(dest_tiles, ybuf, h, gates_tk, sg, su, sd, g, b, alpha):
    T, D = h.shape
    nt, _, tm = dest_tiles.shape
    F = sg.shape[-1]
    row = lambda i: (i, 0)
    fixed = lambda i: (0, 0)
    return pl.pallas_call(
        functools.partial(_combine_kernel, alpha=alpha),
        grid=(nt,),
        in_specs=[pl.BlockSpec((None, TOP_K, tm), lambda i: (i, 0, 0), memory_space=pltpu.SMEM),
                  pl.BlockSpec(memory_space=pl.ANY),
                  pl.BlockSpec((tm, D), row), pl.BlockSpec((tm, TOP_K), row),
                  pl.BlockSpec((D, F), fixed), pl.BlockSpec((D, F), fixed),
                  pl.BlockSpec((F, D), fixed), pl.BlockSpec((1, D), fixed),
                  pl.BlockSpec((1, D), fixed)],
        out_specs=pl.BlockSpec((tm, D), row),
        out_shape=jax.ShapeDtypeStruct((T, D), F32),
        scratch_shapes=[pltpu.VMEM((TOP_K, tm, D), F32), pltpu.SemaphoreType.DMA(())],
        compiler_params=_cparams(1),
        name="moe_combine",
    )(dest_tiles, ybuf, h, gates_tk, sg, su, sd, g, b)


def _moe(h, router_w, router_bias, e_gate, e_up, e_down, s_gate, s_up, s_down, g, b, alpha):
    T, D = h.shape
    idx, gates, rank, counts = _router(h, router_w, router_bias)
    tm = _pick_tile(T, (256, 128))
    nt = T // tm
    sizes = counts[:, 0].astype(I32)
    padded = (sizes + EBLK - 1) // EBLK * EBLK
    pend = jnp.cumsum(padded)
    pstart = pend - padded
    experts = jnp.arange(N_EXPERTS, dtype=I32)
    dest = rank + jnp.sum(
        jnp.where(idx[None] == experts[:, None, None], pstart[:, None, None], 0), axis=0)
    n_blk = (T * TOP_K + N_EXPERTS * (EBLK - 1) + EBLK - 1) // EBLK
    blk_start = jnp.arange(n_blk, dtype=I32) * EBLK
    blk_e = jnp.minimum(jnp.sum((pend[None, :] <= blk_start[:, None]).astype(I32), axis=1),
                        N_EXPERTS - 1)
    dest_tiles = dest.reshape(TOP_K, nt, tm).transpose(1, 0, 2)
    buf = _dispatch(dest_tiles, h, n_blk * EBLK)
    ybuf = _experts(blk_e, buf, e_gate.astype(BF16), e_up.astype(BF16), e_down.astype(BF16))
    return _combine(dest_tiles, ybuf, h, gates.T, s_gate.astype(BF16), s_up.astype(BF16),
                    s_down.astype(BF16), g, b, alpha)


def _layer(layer, depth, x_all, B, L, Bs, Ls, caches, table, w_in, lq1, lk1, lq2, lk2, subln_w,
           sb_w, w_out, ln1_g, ln1_b, router_w, router_bias, e_gate, e_up, e_down, s_gate, s_up,
           s_down, ln2_g, ln2_b):
    alpha = (2.0 * depth) ** 0.25
    lam_init = 0.8 - 0.6 * math.exp(-0.3 * layer)
    Tp = B * L
    D = x_all.shape[1]

    qd, kd, vd, qs, ks, vs, kdb, vdb, ksb, vsb = _in_projection(x_all, w_in.astype(BF16))

    lam = (jnp.exp(jnp.sum(lq1.astype(F32) * lk1.astype(F32)))
           - jnp.exp(jnp.sum(lq2.astype(F32) * lk2.astype(F32))) + lam_init).reshape(1)
    gain_d = subln_w.astype(F32).reshape(1, HEAD_DIM)
    gain_s = sb_w.astype(F32).reshape(1, HEAD_DIM)
    scale_d = 1.0 - lam_init
    past, ck, cv, csk, csv = caches

    od = jnp.concatenate([
        _diff_prompt(lam, qd, kdb, vdb, table, gain_d, scale_d, B, L),
        _diff_sample(lam, qd, kdb, vdb, ck, cv, past, layer, table, gain_d, scale_d, Tp, Bs, Ls)])
    osb = jnp.concatenate([
        _sb_prompt(qs, ksb, vsb, gain_s, B, L),
        _sb_sample(qs, ksb, vsb, csk, csv, past, layer, gain_s, Tp, Bs, Ls)])

    row = lambda v: v.astype(F32).reshape(1, D)
    h = _out_projection(od, osb, x_all, w_out.astype(BF16), row(ln1_g), row(ln1_b), alpha)
    y = _moe(h, router_w, router_bias, e_gate, e_up, e_down, s_gate, s_up, s_down,
             row(ln2_g), row(ln2_b), alpha)
    return y, (kd, vd, ks, vs)


def kernel(x_prompt, x_sample, cache_diff_k, cache_diff_v, cache_sb_k, cache_sb_v, rel_bias_table, w_in, diff_lambda_q1, diff_lambda_k1, diff_lambda_q2, diff_lambda_k2, diff_subln_w, sb_norm_w, w_out, ln1_g, ln1_b, router_w, router_bias, expert_w_gate, expert_w_up, expert_w_down, shared_w_gate, shared_w_up, shared_w_down, ln2_g, ln2_b):
    B, L, D = x_prompt.shape
    Bs, Ls, _ = x_sample.shape
    depth = w_in.shape[0]
    Tp, Ts = B * L, Bs * Ls
    assert Tp % Ls == 0
    x_all = jnp.concatenate([x_prompt.reshape(Tp, D), x_sample.reshape(Ts, D)])
    caches = (cache_diff_k.shape[2],) + tuple(
        c.reshape(-1, HEAD_DIM) for c in (cache_diff_k, cache_diff_v, cache_sb_k, cache_sb_v))
    rows = [[] for _ in range(8)]
    for l in range(depth):
        x_all, new = _layer(
            l, depth, x_all, B, L, Bs, Ls, caches, rel_bias_table, w_in[l], diff_lambda_q1[l],
            diff_lambda_k1[l], diff_lambda_q2[l], diff_lambda_k2[l], diff_subln_w[l],
            sb_norm_w[l], w_out[l], ln1_g[l], ln1_b[l], router_w[l], router_bias[l],
            expert_w_gate[l], expert_w_up[l], expert_w_down[l], shared_w_gate[l],
            shared_w_up[l], shared_w_down[l], ln2_g[l], ln2_b[l])
        for n, a in enumerate(new):
            rows[n].append(a[:Tp].reshape(B, L, N_HEADS, HEAD_DIM))
            rows[4 + n].append(a[Tp:].reshape(Bs, Ls, N_HEADS, HEAD_DIM))
    return (x_all[:Tp].reshape(B, L, D), x_all[Tp:].reshape(Bs, Ls, D),
            *[jnp.stack(r) for r in rows])
```
